```python
import jax, jax.numpy as jnp
from jax import lax
import numpy as np

D_MODEL = 1024
BATCH = 8
SEQ = 4096
DEPTH = 1
DEC_BATCH = 32
DEC_SEQ = 64
PAST_LEN = 1024

CHUNK = 64
D_MIX = 2 * D_MODEL
D_A = D_MIX // 2
GMLP_HEADS = 8
GMLP_HEAD_DIM = D_A // GMLP_HEADS
GMLP_CHUNK = 128
D_B = D_MIX - D_A
SSM_HEAD_DIM = 64
SSM_HEADS = D_B // SSM_HEAD_DIM
SSM_GROUPS = 2
SSM_STATE = 128
GN = SSM_GROUPS * SSM_STATE
CONV_W = 4
CONV_DIM = D_B + 2 * GN
SSD_CHUNK = CHUNK
D_IN_PROJ = 3 * D_A + D_B + CONV_DIM + SSM_HEADS
SPLITS = (D_A, 2 * D_A, 3 * D_A, 3 * D_A + D_B, 3 * D_A + D_B + CONV_DIM)
EPS = 1e-5

kernel_name = "hybrid_gmlp_ssd_streaming_step"


def rms_norm(x, w):
    xf = x.astype(jnp.float32)
    y = xf * lax.rsqrt(jnp.mean(xf * xf, axis=-1, keepdims=True) + EPS)
    return y.astype(x.dtype) * w


def layer_norm(x, w, b):
    xf = x.astype(jnp.float32)
    mu = jnp.mean(xf, axis=-1, keepdims=True)
    var = jnp.mean(jnp.square(xf - mu), axis=-1, keepdims=True)
    return ((xf - mu) * lax.rsqrt(var + EPS)).astype(x.dtype) * w + b


def gmlp_branch(u, v, g, ln_w, ln_b, ws, bs, chunk_len):
    bsz, seqlen, _ = u.shape
    n_chunks = seqlen // chunk_len
    vn = layer_norm(v, ln_w, ln_b)
    blk = jnp.arange(GMLP_CHUNK) // CHUNK
    mask = (blk[None, :] <= blk[:, None])[:chunk_len, :chunk_len]
    w_mask = jnp.where(mask[None], ws[:, :chunk_len, :chunk_len], 0.0)
    vr = vn.reshape(bsz, n_chunks, chunk_len, GMLP_HEADS, GMLP_HEAD_DIM)
    mixed = jnp.einsum('hts,bcshd->bcthd', w_mask, vr) + bs[:, :chunk_len].T[:, :, None]
    ur = u.reshape(bsz, n_chunks, chunk_len, GMLP_HEADS, GMLP_HEAD_DIM)
    out = (ur * mixed).reshape(bsz, seqlen, D_A) * jax.nn.silu(g)
    return out, vn


def causal_conv(xbc, conv_state, w, b):
    seqlen = xbc.shape[1]
    xpad = jnp.concatenate([conv_state, xbc], axis=1)
    y = xpad[:, 0:seqlen] * w[0]
    for k in range(1, CONV_W):
        y = y + xpad[:, k:k + seqlen] * w[k]
    return jax.nn.silu(y + b), xpad[:, -(CONV_W - 1):]


def ssd_scan(x, dt, a, bm, cm, h0, chunk):
    bsz, seqlen, n_heads, hd = x.shape
    nc = seqlen // chunk
    e = n_heads // SSM_GROUPS
    x = x.reshape(bsz, nc, chunk, SSM_GROUPS, e, hd)
    dt = dt.reshape(bsz, nc, chunk, SSM_GROUPS, e)
    bm = bm.reshape(bsz, nc, chunk, SSM_GROUPS, SSM_STATE)
    cm = cm.reshape(bsz, nc, chunk, SSM_GROUPS, SSM_STATE)
    a_cum = jnp.cumsum(dt.astype(jnp.float32) * a.reshape(SSM_GROUPS, e).astype(jnp.float32), axis=2)
    seg = a_cum[:, :, :, None] - a_cum[:, :, None]
    causal = jnp.tril(jnp.ones((chunk, chunk), dtype=bool))[:, :, None, None]
    l_dec = jnp.exp(jnp.where(causal, seg, -jnp.inf)).astype(x.dtype)
    xdt = x * dt[..., None]
    cb = jnp.einsum('bclgn,bcsgn->bclsg', cm, bm)
    y_diag = jnp.einsum('bclsg,bclsge,bcsgep->bclgep', cb, l_dec, xdt)
    decay_to_end = jnp.exp(a_cum[:, :, -1:] - a_cum).astype(x.dtype)
    states = jnp.einsum('bclgn,bclge,bclgep->bcgepn', bm, decay_to_end, xdt)
    chunk_decay = jnp.exp(a_cum[:, :, -1]).astype(x.dtype)

    def step(h, inp):
        s, d = inp
        return h * d[..., None, None] + s, h

    h_init = h0.reshape(bsz, SSM_GROUPS, e, hd, SSM_STATE)
    h_last, h_in = lax.scan(step, h_init, (jnp.moveaxis(states, 1, 0), jnp.moveaxis(chunk_decay, 1, 0)))
    h_in = jnp.moveaxis(h_in, 0, 1)
    y_off = jnp.einsum('bclgn,bcgepn,bclge->bclgep', cm, h_in, jnp.exp(a_cum).astype(x.dtype))
    y = (y_diag + y_off).reshape(bsz, seqlen, n_heads, hd)
    return y, h_last.reshape(bsz, n_heads, hd, SSM_STATE)


def mamba_branch(z, xbc, dt_raw, conv_state, h0, conv_w, conv_b, dt_bias, a_log, d_skip, norm_w, chunk):
    bsz, seqlen, _ = z.shape
    xbc, new_conv = causal_conv(xbc, conv_state, conv_w, conv_b)
    xs = xbc[..., :D_B].reshape(bsz, seqlen, SSM_HEADS, SSM_HEAD_DIM)
    bm = xbc[..., D_B:D_B + GN].reshape(bsz, seqlen, SSM_GROUPS, SSM_STATE)
    cm = xbc[..., D_B + GN:].reshape(bsz, seqlen, SSM_GROUPS, SSM_STATE)
    dt = jax.nn.softplus(dt_raw + dt_bias)
    a = -jnp.exp(a_log)
    y, h_last = ssd_scan(xs, dt, a, bm, cm, h0, chunk)
    y = (y + xs * d_skip[:, None]).reshape(bsz, seqlen, D_B)
    y = rms_norm(y * jax.nn.silu(z), norm_w)
    return y, h_last, new_conv


def mixer_layer(x, conv_state, h0, gmlp_chunk, ssd_chunk, norm_in_w, w_in, gmlp_ln_w, gmlp_ln_b,
                gmlp_ws, gmlp_bs, conv_w, conv_b, dt_bias, a_log, d_skip, ssm_norm_w, w_out):
    h = rms_norm(x, norm_in_w) @ w_in
    u, v, g, z, xbc, dt_raw = jnp.split(h, SPLITS, axis=-1)
    a_out, vn = gmlp_branch(u, v, g, gmlp_ln_w, gmlp_ln_b, gmlp_ws, gmlp_bs, gmlp_chunk)
    b_out, h_last, new_conv = mamba_branch(z, xbc, dt_raw, conv_state, h0, conv_w, conv_b,
                                           dt_bias, a_log, d_skip, ssm_norm_w, ssd_chunk)
    y = x + jnp.concatenate([a_out, b_out], axis=-1) @ w_out
    return y, vn, h_last, new_conv


def setup_inputs(seed: int = 0) -> dict:
    key = jax.random.key(seed)
    ks = jax.random.split(key, 20)
    nrm = jax.random.normal
    dt0 = jnp.exp(jax.random.uniform(ks[12], (DEPTH, SSM_HEADS)) * (np.log(0.1) - np.log(0.001)) + np.log(0.001))
    return {
        "x_prompt": nrm(ks[0], (BATCH, SEQ, D_MODEL), jnp.float32),
        "x_sample": nrm(ks[1], (DEC_BATCH, DEC_SEQ, D_MODEL), jnp.float32),
        "state_ssm": 0.1 * nrm(ks[2], (DEPTH, DEC_BATCH, SSM_HEADS, SSM_HEAD_DIM, SSM_STATE), jnp.float32),
        "state_conv": nrm(ks[3], (DEPTH, DEC_BATCH, CONV_W - 1, CONV_DIM), jnp.float32),
        "norm_in_w": 1.0 + 0.02 * nrm(ks[4], (DEPTH, D_MODEL), jnp.float32),
        "w_in": nrm(ks[5], (DEPTH, D_MODEL, D_IN_PROJ), jnp.float32) * D_MODEL ** -0.5,
        "gmlp_ln_w": 1.0 + 0.02 * nrm(ks[6], (DEPTH, D_A), jnp.float32),
        "gmlp_ln_b": 0.02 * nrm(ks[7], (DEPTH, D_A), jnp.float32),
        "gmlp_ws": nrm(ks[8], (DEPTH, GMLP_HEADS, GMLP_CHUNK, GMLP_CHUNK), jnp.float32) * GMLP_CHUNK ** -0.5,
        "gmlp_bs": 1.0 + 0.02 * nrm(ks[9], (DEPTH, GMLP_HEADS, GMLP_CHUNK), jnp.float32),
        "conv_w": nrm(ks[10], (DEPTH, CONV_W, CONV_DIM), jnp.float32) * CONV_W ** -0.5,
        "conv_b": 0.02 * nrm(ks[11], (DEPTH, CONV_DIM), jnp.float32),
        "dt_bias": dt0 + jnp.log(-jnp.expm1(-dt0)),
        "a_log": jnp.log(jax.random.uniform(ks[13], (DEPTH, SSM_HEADS), minval=1.0, maxval=16.0)),
        "d_skip": 1.0 + 0.02 * nrm(ks[14], (DEPTH, SSM_HEADS), jnp.float32),
        "ssm_norm_w": 1.0 + 0.02 * nrm(ks[15], (DEPTH, D_B), jnp.float32),
        "w_out": nrm(ks[16], (DEPTH, D_MIX, D_MODEL), jnp.float32) * D_MIX ** -0.5,
        "norm_f_w": 1.0 + 0.02 * nrm(ks[17], (D_MODEL,), jnp.float32),
    }


def reference(x_prompt, x_sample, state_ssm, state_conv, norm_in_w, w_in, gmlp_ln_w, gmlp_ln_b,
              gmlp_ws, gmlp_bs, conv_w, conv_b, dt_bias, a_log, d_skip, ssm_norm_w, w_out, norm_f_w):
    yp, ys = x_prompt, x_sample
    bp, dec_len = yp.shape[0], ys.shape[1]
    ssm_p, conv_p, ssm_s, conv_s, v_s = [], [], [], [], []
    for i in range(DEPTH):
        lw = (norm_in_w[i], w_in[i], gmlp_ln_w[i], gmlp_ln_b[i], gmlp_ws[i], gmlp_bs[i], conv_w[i],
              conv_b[i], dt_bias[i], a_log[i], d_skip[i], ssm_norm_w[i], w_out[i])
        conv0 = jnp.zeros((bp, CONV_W - 1, CONV_DIM), yp.dtype)
        h0 = jnp.zeros((bp, SSM_HEADS, SSM_HEAD_DIM, SSM_STATE), yp.dtype)
        yp, _, hp, cp = mixer_layer(yp, conv0, h0, GMLP_CHUNK, SSD_CHUNK, *lw)
        ys, vs, hs, cs = mixer_layer(ys, state_conv[i], state_ssm[i], dec_len, dec_len, *lw)
        ssm_p.append(hp)
        conv_p.append(cp)
        ssm_s.append(hs)
        conv_s.append(cs)
        v_s.append(vs)
    y_prompt = rms_norm(yp, norm_f_w)
    y_sample = rms_norm(ys, norm_f_w)
    return (y_prompt, y_sample, jnp.stack(ssm_p), jnp.stack(conv_p), jnp.stack(ssm_s), jnp.stack(conv_s), jnp.stack(v_s))
```

```python
import functools

import jax
import jax.numpy as jnp
from jax import lax
from jax.experimental import pallas as pl
from jax.experimental.pallas import tpu as pltpu

D_MODEL = 1024
CHUNK = 64
D_A = 1024
GMLP_HEADS = 8
GMLP_HEAD_DIM = 128
GMLP_CHUNK = 128
D_B = 1024
SSM_HEADS = 16
SSM_HEAD_DIM = 64
SSM_GROUPS = 2
SSM_STATE = 128
GN = SSM_GROUPS * SSM_STATE
CONV_W = 4
CONV_DIM = D_B + 2 * GN
EPS = 1e-5

TILE = 256
N_CHUNKS = TILE // CHUNK
HIST = 8
GROUP_W = D_B // SSM_GROUPS
BLOCK_HEADS = 4
BLOCK_W = BLOCK_HEADS * SSM_HEAD_DIM

OFF_U, OFF_V, OFF_G, OFF_Z, OFF_XBC, OFF_DT = 0, D_A, 2 * D_A, 3 * D_A, 3 * D_A + D_B, 3 * D_A + D_B + CONV_DIM
W_IN_COLS = OFF_DT + D_B

V7X_VMEM_LIMIT_BYTES = 56 * 1024 * 1024

F32 = jnp.float32
BF16 = jnp.bfloat16


def _silu(x):
    return x * jax.nn.sigmoid(x)


def _softplus(x):
    return jnp.maximum(x, 0.0) + jnp.log1p(jnp.exp(-jnp.abs(x)))


def _mixer_kernel(x_ref, ssm_in_ref, conv_in_ref, nin_ref, win_ref, lnw_ref, lnb_ref, ws_ref, bsb_ref,
                  cw_ref, cb_ref, dtb_ref, alog_ref, dsk_ref, snw_ref, wout_ref, nf_ref,
                  *out_and_scratch, streaming):
    if streaming:
        y_ref, ssm_out_ref, conv_out_ref = out_and_scratch[:3]
        vn_out_ref = None
        scratch = out_and_scratch[3:]
    else:
        y_ref, ssm_out_ref, conv_out_ref, vn_out_ref = out_and_scratch[:4]
        scratch = out_and_scratch[4:]
    st_ref, xbc_ref, ug_ref, vnb_ref, zs_ref, dt_ref, mix_ref = scratch

    if streaming:
        tile_idx = pl.program_id(1)
        n_tiles = pl.num_programs(1)

    x = x_ref[...]
    xn = x * lax.rsqrt(jnp.mean(x * x, axis=-1, keepdims=True) + EPS) * nin_ref[...]
    xnb = xn.astype(BF16)

    def proj(lo, hi):
        return jnp.dot(xnb, win_ref[:, lo:hi], preferred_element_type=F32)

    u = proj(OFF_U, OFF_U + D_A)
    g = proj(OFF_G, OFF_G + D_A)
    ug_ref[...] = u * _silu(g)

    v = proj(OFF_V, OFF_V + D_A)
    mu = jnp.mean(v, axis=-1, keepdims=True)
    vc = v - mu
    var = jnp.mean(vc * vc, axis=-1, keepdims=True)
    vn = vc * lax.rsqrt(var + EPS) * lnw_ref[...] + lnb_ref[...]
    if vn_out_ref is not None:
        vn_out_ref[...] = vn
    vnb_ref[...] = vn.astype(BF16)

    zs_ref[...] = _silu(proj(OFF_Z, OFF_Z + D_B))
    dt_ref[...] = _softplus(proj(OFF_DT, OFF_DT + D_B) + dtb_ref[...])

    if streaming:
        @pl.when(tile_idx == 0)
        def _():
            xbc_ref[0, HIST - (CONV_W - 1):HIST, :] = conv_in_ref[0]

        @pl.when(tile_idx > 0)
        def _():
            xbc_ref[0, 0:HIST, :] = xbc_ref[N_CHUNKS - 1, CHUNK:CHUNK + HIST, :]

    xbc = proj(OFF_XBC, OFF_XBC + CONV_DIM)
    for c in range(N_CHUNKS):
        xbc_ref[c, HIST:HIST + CHUNK, :] = xbc[c * CHUNK:(c + 1) * CHUNK]
        if streaming:
            if c > 0:
                xbc_ref[c, 0:HIST, :] = xbc_ref[c - 1, CHUNK:CHUNK + HIST, :]
        else:
            xbc_ref[c, HIST - (CONV_W - 1):HIST, :] = conv_in_ref[c]

    gl = GMLP_CHUNK if streaming else CHUNK
    blk_t = lax.broadcasted_iota(jnp.int32, (gl, gl), 0) // CHUNK
    blk_s = lax.broadcasted_iota(jnp.int32, (gl, gl), 1) // CHUNK
    w_keep = blk_s <= blk_t
    for h in range(GMLP_HEADS):
        cols = slice(h * GMLP_HEAD_DIM, (h + 1) * GMLP_HEAD_DIM)
        wm = jnp.where(w_keep, ws_ref[h, 0:gl, 0:gl], 0.0).astype(BF16)
        bias = bsb_ref[0:gl, cols]
        for k in range(TILE // gl):
            rows = slice(k * gl, (k + 1) * gl)
            mixed = jnp.dot(wm, vnb_ref[rows, cols], preferred_element_type=F32) + bias
            mix_ref[rows, cols] = (ug_ref[rows, cols] * mixed).astype(BF16)

    a_b = -jnp.exp(alog_ref[...])
    d_b = dsk_ref[...]
    cw = cw_ref[...]
    cb = cb_ref[...]
    snw = snw_ref[...]

    row = lax.broadcasted_iota(jnp.int32, (CHUNK, D_B), 0)
    pos = lax.broadcasted_iota(jnp.int32, (CHUNK, D_B), 1) % SSM_HEAD_DIM
    causal = row >= pos
    upto = row <= pos
    li = lax.broadcasted_iota(jnp.int32, (2 * CHUNK, 4 * CHUNK), 0)
    ji = lax.broadcasted_iota(jnp.int32, (2 * CHUNK, 4 * CHUNK), 1)
    tri = (ji % CHUNK) <= (li % CHUNK)
    is_ones_blk = (ji // CHUNK) % 2 == 1
    cum_lhs = jnp.where(is_ones_blk, jnp.where(li < CHUNK, -1.0, 0.0), jnp.where(tri, 1.0, 0.0)).astype(BF16)
    blk_r = lax.broadcasted_iota(jnp.int32, (BLOCK_W, BLOCK_W), 0) // SSM_HEAD_DIM
    blk_c = lax.broadcasted_iota(jnp.int32, (BLOCK_W, BLOCK_W), 1) // SSM_HEAD_DIM
    same_head = blk_r == blk_c

    if streaming:
        @pl.when(tile_idx == 0)
        def _():
            st_ref[...] = ssm_in_ref[0].T

    for c in range(N_CHUNKS):
        rows = slice(c * CHUNK, (c + 1) * CHUNK)
        if not streaming:
            st_ref[...] = ssm_in_ref[c].T

        base = HIST - (CONV_W - 1)
        acc = xbc_ref[c, base:base + CHUNK, :] * cw[0:1]
        for k in range(1, CONV_W):
            acc = acc + xbc_ref[c, base + k:base + k + CHUNK, :] * cw[k:k + 1]
        xc = _silu(acc + cb)
        xs = xc[:, :D_B]
        bmat = xc[:, D_B:D_B + GN].astype(BF16)
        cmat = xc[:, D_B + GN:].astype(BF16)

        dt = dt_ref[rows, :]
        da = dt * a_b
        da_hi = da.astype(BF16)
        da_lo = (da - da_hi.astype(F32)).astype(BF16)
        zero = jnp.zeros_like(da_hi)
        cum_rhs = jnp.concatenate(
            [da_hi, jnp.where(upto, da_hi, zero), da_lo, jnp.where(upto, da_lo, zero)], axis=0)
        seg_cum = jnp.dot(cum_lhs, cum_rhs, preferred_element_type=F32)
        seg = seg_cum[:CHUNK]
        cum = seg_cum[CHUNK:]
        decay = jnp.where(causal, jnp.exp(seg), 0.0)

        cb_tiled = []
        for gi in range(SSM_GROUPS):
            ncols = slice(gi * SSM_STATE, (gi + 1) * SSM_STATE)
            b_rep = jnp.concatenate([bmat[:, ncols]] * (GROUP_W // CHUNK), axis=0)
            cb_tiled.append(lax.dot_general(cmat[:, ncols], b_rep, (((1,), (1,)), ((), ())),
                                            preferred_element_type=F32))
        m = (jnp.concatenate(cb_tiled, axis=1) * decay).astype(BF16)

        xdt = xs * dt
        xdt_b = xdt.astype(BF16)
        y_parts = []
        for j in range(D_B // BLOCK_W):
            cols = slice(j * BLOCK_W, (j + 1) * BLOCK_W)
            stacked = jnp.concatenate([xdt_b[:, cols]] * BLOCK_HEADS, axis=0)
            rhs = jnp.where(same_head, stacked, jnp.zeros_like(stacked))
            y_parts.append(jnp.dot(m[:, cols], rhs, preferred_element_type=F32))
        y = jnp.concatenate(y_parts, axis=1)

        st = st_ref[...]
        st_b = st.astype(BF16)
        cum_last = cum[CHUNK - 1:CHUNK, :]
        w_in_state = (xdt * jnp.exp(cum_last - cum)).astype(BF16)
        y_off = []
        st_new = []
        for gi in range(SSM_GROUPS):
            ncols = slice(gi * SSM_STATE, (gi + 1) * SSM_STATE)
            hcols = slice(gi * GROUP_W, (gi + 1) * GROUP_W)
            y_off.append(jnp.dot(cmat[:, ncols], st_b[:, hcols], preferred_element_type=F32))
            st_new.append(lax.dot_general(bmat[:, ncols], w_in_state[:, hcols], (((0,), (0,)), ((), ())),
                                          preferred_element_type=F32))
        y = y + jnp.concatenate(y_off, axis=1) * jnp.exp(cum) + xs * d_b
        st_next = st * jnp.exp(cum_last) + jnp.concatenate(st_new, axis=1)
        st_ref[...] = st_next

        yz = y * zs_ref[rows, :]
        yn = yz * lax.rsqrt(jnp.mean(yz * yz, axis=-1, keepdims=True) + EPS) * snw
        mix_ref[rows, D_A:] = yn.astype(BF16)

        if not streaming:
            ssm_out_ref[c] = st_next.T
            conv_out_ref[c] = xbc_ref[c, CHUNK + HIST - (CONV_W - 1):CHUNK + HIST, :]

    if streaming:
        @pl.when(tile_idx == n_tiles - 1)
        def _():
            ssm_out_ref[0] = st_ref[...].T
            conv_out_ref[0] = xbc_ref[N_CHUNKS - 1, CHUNK + HIST - (CONV_W - 1):CHUNK + HIST, :]

    o = x + jnp.dot(mix_ref[...], wout_ref[...], preferred_element_type=F32)
    y_ref[...] = o * lax.rsqrt(jnp.mean(o * o, axis=-1, keepdims=True) + EPS) * nf_ref[...]


def _run_path(x, ssm_in, conv_in, weights, *, streaming):
    n_seq, seq_len, _ = x.shape
    n_rows = n_seq * seq_len
    x2 = x.reshape(n_rows, D_MODEL)
    if streaming:
        assert seq_len % TILE == 0
        tiles_per_seq = seq_len // TILE
        grid = (n_seq, tiles_per_seq)
        row_map = lambda b, i: (b * tiles_per_seq + i, 0)
        seq_map = lambda b, i: (b, 0, 0)
        const2 = lambda b, i: (0, 0)
        const3 = lambda b, i: (0, 0, 0)
        seq_blk = 1
    else:
        assert seq_len == CHUNK and n_rows % TILE == 0
        grid = (n_rows // TILE,)
        row_map = lambda i: (i, 0)
        seq_map = lambda i: (i, 0, 0)
        const2 = lambda i: (0, 0)
        const3 = lambda i: (0, 0, 0)
        seq_blk = N_CHUNKS

    def const_spec(arr):
        if arr.ndim == 2:
            return pl.BlockSpec(arr.shape, const2)
        return pl.BlockSpec(arr.shape, const3)

    in_specs = [
        pl.BlockSpec((TILE, D_MODEL), row_map),
        pl.BlockSpec((seq_blk, D_B, SSM_STATE), seq_map),
        pl.BlockSpec((seq_blk, CONV_W - 1, CONV_DIM), seq_map),
    ] + [const_spec(w) for w in weights]

    out_shape = [
        jax.ShapeDtypeStruct((n_rows, D_MODEL), F32),
        jax.ShapeDtypeStruct((n_seq, D_B, SSM_STATE), F32),
        jax.ShapeDtypeStruct((n_seq, CONV_W - 1, CONV_DIM), F32),
    ]
    out_specs = [
        pl.BlockSpec((TILE, D_MODEL), row_map),
        pl.BlockSpec((seq_blk, D_B, SSM_STATE), seq_map),
        pl.BlockSpec((seq_blk, CONV_W - 1, CONV_DIM), seq_map),
    ]
    if not streaming:
        out_shape.append(jax.ShapeDtypeStruct((n_rows, D_A), F32))
        out_specs.append(pl.BlockSpec((TILE, D_A), row_map))

    scratch_shapes = [
        pltpu.VMEM((SSM_STATE, D_B), F32),
        pltpu.VMEM((N_CHUNKS, HIST + CHUNK, CONV_DIM), F32),
        pltpu.VMEM((TILE, D_A), F32),
        pltpu.VMEM((TILE, D_A), BF16),
        pltpu.VMEM((TILE, D_B), F32),
        pltpu.VMEM((TILE, D_B), F32),
        pltpu.VMEM((TILE, D_A + D_B), BF16),
    ]

    return pl.pallas_call(
        functools.partial(_mixer_kernel, streaming=streaming),
        grid=grid,
        in_specs=in_specs,
        out_specs=out_specs,
        out_shape=out_shape,
        scratch_shapes=scratch_shapes,
        compiler_params=pltpu.CompilerParams(
            dimension_semantics=("arbitrary",) * len(grid),
            vmem_limit_bytes=V7X_VMEM_LIMIT_BYTES,
        ),
        name="mixer_prompt" if streaming else "mixer_sample",
    )(x2, ssm_in, conv_in, *weights)


def kernel(x_prompt, x_sample, state_ssm, state_conv, norm_in_w, w_in, gmlp_ln_w, gmlp_ln_b, gmlp_ws, gmlp_bs,
           conv_w, conv_b, dt_bias, a_log, d_skip, ssm_norm_w, w_out, norm_f_w):
    depth = w_in.shape[0]
    assert depth == 1
    bp, seq, _ = x_prompt.shape
    bs, dec_len, _ = x_sample.shape

    def head_bcast(p):
        return jnp.repeat(p, SSM_HEAD_DIM)[None, :]

    i = 0
    w = w_in[i]
    w_dt = jnp.repeat(w[:, OFF_DT:], SSM_HEAD_DIM, axis=1)
    w_all = jnp.concatenate([w[:, :OFF_DT], w_dt], axis=1).astype(BF16)
    weights = (
        norm_in_w[i][None, :], w_all, gmlp_ln_w[i][None, :], gmlp_ln_b[i][None, :], gmlp_ws[i],
        jnp.repeat(gmlp_bs[i].T, GMLP_HEAD_DIM, axis=1),
        conv_w[i], conv_b[i][None, :], head_bcast(dt_bias[i]), head_bcast(a_log[i]), head_bcast(d_skip[i]),
        ssm_norm_w[i][None, :], w_out[i].astype(BF16), norm_f_w[None, :],
    )

    zero_ssm = jnp.zeros((bp, D_B, SSM_STATE), F32)
    zero_conv = jnp.zeros((bp, CONV_W - 1, CONV_DIM), F32)
    yp, ssm_p, conv_p = _run_path(x_prompt, zero_ssm, zero_conv, weights, streaming=True)
    ys, ssm_s, conv_s, vn_s = _run_path(
        x_sample, state_ssm[i].reshape(bs, D_B, SSM_STATE), state_conv[i], weights, streaming=False)

    return (
        yp.reshape(bp, seq, D_MODEL),
        ys.reshape(bs, dec_len, D_MODEL),
        ssm_p.reshape(1, bp, SSM_HEADS, SSM_HEAD_DIM, SSM_STATE),
        conv_p[None],
        ssm_s.reshape(1, bs, SSM_HEADS, SSM_HEAD_DIM, SSM_STATE),
        conv_s[None],
        vn_s.reshape(1, bs, dec_len, D_A),
    )
```

```python
import functools

import jax
import jax.numpy as jnp
from jax import lax
from jax.experimental import pallas as pl
from jax.experimental.pallas import tpu as pltpu

D_MODEL = 1024
CHUNK = 64
D_A = 1024
GMLP_HEADS = 8
GMLP_HEAD_DIM = 128
GMLP_CHUNK = 128
GMLP_PAIR_W = 2 * GMLP_HEAD_DIM
D_B = 1024
SSM_HEADS = 16
SSM_HEAD_DIM = 64
SSM_GROUPS = 2
SSM_STATE = 128
GN = SSM_GROUPS * SSM_STATE
CONV_W = 4
CONV_DIM = D_B + 2 * GN
EPS = 1e-5

TILE = 256
N_CHUNKS = TILE // CHUNK
HIST = 8
GROUP_W = D_B // SSM_GROUPS
BLOCK_HEADS = 4
BLOCK_W = BLOCK_HEADS * SSM_HEAD_DIM

OFF_U, OFF_V, OFF_G, OFF_Z, OFF_XBC, OFF_DT = 0, D_A, 2 * D_A, 3 * D_A, 3 * D_A + D_B, 3 * D_A + D_B + CONV_DIM
W_IN_COLS = OFF_DT + D_B

V7X_VMEM_LIMIT_BYTES = 56 * 1024 * 1024

F32 = jnp.float32
BF16 = jnp.bfloat16


def _silu(x):
    return x * jax.nn.sigmoid(x)


def _softplus(x):
    return jnp.maximum(x, 0.0) + jnp.log1p(jnp.exp(-jnp.abs(x)))


def _mixer_kernel(x_ref, ssm_in_ref, conv_in_ref, nin_ref, win_ref, lnw_ref, lnb_ref, ws_ref, bsb_ref,
                  cw_ref, cb_ref, dtb_ref, alog_ref, dsk_ref, snw_ref, wout_ref, nf_ref,
                  *out_and_scratch, streaming):
    if streaming:
        y_ref, ssm_out_ref, conv_out_ref = out_and_scratch[:3]
        vn_out_ref = None
        scratch = out_and_scratch[3:]
    else:
        y_ref, ssm_out_ref, conv_out_ref, vn_out_ref = out_and_scratch[:4]
        scratch = out_and_scratch[4:]
    st_ref, xbc_ref, oa_ref, vnb_ref, zs_ref, dt_ref, mixa_ref, mixb_ref = scratch

    if streaming:
        tile_idx = pl.program_id(1)
        n_tiles = pl.num_programs(1)

    x = x_ref[...]
    xn = x * lax.rsqrt(jnp.mean(x * x, axis=-1, keepdims=True) + EPS) * nin_ref[...]
    xnb = xn.astype(BF16)

    def proj(lo, hi):
        return jnp.dot(xnb, win_ref[:, lo:hi], preferred_element_type=F32)

    dt_ref[...] = _softplus(proj(OFF_DT, OFF_DT + D_B) + dtb_ref[...])
    zs_ref[...] = _silu(proj(OFF_Z, OFF_Z + D_B))

    if streaming:
        @pl.when(tile_idx == 0)
        def _():
            xbc_ref[0, HIST - (CONV_W - 1):HIST, :] = conv_in_ref[0]

        @pl.when(tile_idx > 0)
        def _():
            xbc_ref[0, 0:HIST, :] = xbc_ref[N_CHUNKS - 1, CHUNK:CHUNK + HIST, :]

    xbc = proj(OFF_XBC, OFF_XBC + CONV_DIM)
    for c in range(N_CHUNKS):
        xbc_ref[c, HIST:HIST + CHUNK, :] = xbc[c * CHUNK:(c + 1) * CHUNK]
        if streaming:
            if c > 0:
                xbc_ref[c, 0:HIST, :] = xbc_ref[c - 1, CHUNK:CHUNK + HIST, :]
        else:
            xbc_ref[c, HIST - (CONV_W - 1):HIST, :] = conv_in_ref[c]

    gl = GMLP_CHUNK if streaming else CHUNK
    blk_t = lax.broadcasted_iota(jnp.int32, (gl, gl), 0) // CHUNK
    blk_s = lax.broadcasted_iota(jnp.int32, (gl, gl), 1) // CHUNK
    w_keep = blk_s <= blk_t

    def gmlp_norm_v():
        v = proj(OFF_V, OFF_V + D_A)
        mu = jnp.mean(v, axis=-1, keepdims=True)
        vc = v - mu
        var = jnp.mean(vc * vc, axis=-1, keepdims=True)
        vn = vc * lax.rsqrt(var + EPS) * lnw_ref[...] + lnb_ref[...]
        if vn_out_ref is not None:
            vn_out_ref[...] = vn
        vnb_ref[...] = vn.astype(BF16)

    def gmlp_head_pair(j):
        lo = j * GMLP_PAIR_W
        u = proj(OFF_U + lo, OFF_U + lo + GMLP_PAIR_W)
        g = proj(OFF_G + lo, OFF_G + lo + GMLP_PAIR_W)
        ug = u * _silu(g)
        for hh in range(2):
            h = 2 * j + hh
            cols = slice(h * GMLP_HEAD_DIM, (h + 1) * GMLP_HEAD_DIM)
            wm = jnp.where(w_keep, ws_ref[h, 0:gl, 0:gl], 0.0).astype(BF16)
            bias = bsb_ref[0:gl, cols]
            for k in range(TILE // gl):
                rows = slice(k * gl, (k + 1) * gl)
                mixed = jnp.dot(wm, vnb_ref[rows, cols], preferred_element_type=F32) + bias
                ug_h = ug[rows, hh * GMLP_HEAD_DIM:(hh + 1) * GMLP_HEAD_DIM]
                mixa_ref[rows, cols] = (ug_h * mixed).astype(BF16)

    def out_proj_gmlp():
        oa_ref[...] = jnp.dot(mixa_ref[...], wout_ref[0:D_A, :], preferred_element_type=F32)

    side_work = {
        0: (gmlp_norm_v,),
        1: (functools.partial(gmlp_head_pair, 0), functools.partial(gmlp_head_pair, 1)),
        2: (functools.partial(gmlp_head_pair, 2), functools.partial(gmlp_head_pair, 3)),
        3: (out_proj_gmlp,),
    }

    a_b = -jnp.exp(alog_ref[...])
    d_b = dsk_ref[...]
    cw = cw_ref[...]
    cb = cb_ref[...]
    snw = snw_ref[...]

    row = lax.broadcasted_iota(jnp.int32, (CHUNK, D_B), 0)
    pos = lax.broadcasted_iota(jnp.int32, (CHUNK, D_B), 1) % SSM_HEAD_DIM
    causal = row >= pos
    upto = jnp.where(row <= pos, 1.0, 0.0).astype(BF16)
    li = lax.broadcasted_iota(jnp.int32, (2 * CHUNK, 4 * CHUNK), 0)
    ji = lax.broadcasted_iota(jnp.int32, (2 * CHUNK, 4 * CHUNK), 1)
    tri = (ji % CHUNK) <= (li % CHUNK)
    is_ones_blk = (ji // CHUNK) % 2 == 1
    cum_lhs = jnp.where(is_ones_blk, jnp.where(li < CHUNK, -1.0, 0.0), jnp.where(tri, 1.0, 0.0)).astype(BF16)
    blk_r = lax.broadcasted_iota(jnp.int32, (BLOCK_W, BLOCK_W), 0) // SSM_HEAD_DIM
    blk_c = lax.broadcasted_iota(jnp.int32, (BLOCK_W, BLOCK_W), 1) // SSM_HEAD_DIM
    same_head = blk_r == blk_c

    if streaming:
        @pl.when(tile_idx == 0)
        def _():
            st_ref[...] = ssm_in_ref[0].T

    for c in range(N_CHUNKS):
        for work in side_work[c]:
            work()
        rows = slice(c * CHUNK, (c + 1) * CHUNK)
        if not streaming:
            st_ref[...] = ssm_in_ref[c].T

        base = HIST - (CONV_W - 1)
        acc = xbc_ref[c, base:base + CHUNK, :] * cw[0:1]
        for k in range(1, CONV_W):
            acc = acc + xbc_ref[c, base + k:base + k + CHUNK, :] * cw[k:k + 1]
        xc = _silu(acc + cb)
        xs = xc[:, :D_B]
        bmat = xc[:, D_B:D_B + GN].astype(BF16)
        cmat = xc[:, D_B + GN:].astype(BF16)

        dt = dt_ref[rows, :]
        da = dt * a_b
        da_hi = da.astype(BF16)
        da_lo = (da - da_hi.astype(F32)).astype(BF16)
        cum_rhs = jnp.concatenate([da_hi, da_hi * upto, da_lo, da_lo * upto], axis=0)
        seg_cum = jnp.dot(cum_lhs, cum_rhs, preferred_element_type=F32)
        seg = seg_cum[:CHUNK]
        cum = seg_cum[CHUNK:]
        decay = jnp.where(causal, jnp.exp(seg), 0.0)

        cb_tiled = []
        for gi in range(SSM_GROUPS):
            ncols = slice(gi * SSM_STATE, (gi + 1) * SSM_STATE)
            b_rep = jnp.concatenate([bmat[:, ncols]] * (GROUP_W // CHUNK), axis=0)
            cb_tiled.append(lax.dot_general(cmat[:, ncols], b_rep, (((1,), (1,)), ((), ())),
                                            preferred_element_type=F32))
        m = (jnp.concatenate(cb_tiled, axis=1) * decay).astype(BF16)

        xdt = xs * dt
        xdt_b = xdt.astype(BF16)
        y_parts = []
        for j in range(D_B // BLOCK_W):
            cols = slice(j * BLOCK_W, (j + 1) * BLOCK_W)
            stacked = jnp.concatenate([xdt_b[:, cols]] * BLOCK_HEADS, axis=0)
            rhs = jnp.where(same_head, stacked, jnp.zeros_like(stacked))
            y_parts.append(jnp.dot(m[:, cols], rhs, preferred_element_type=F32))
        y = jnp.concatenate(y_parts, axis=1)

        st = st_ref[...]
        st_b = st.astype(BF16)
        cum_last = cum[CHUNK - 1:CHUNK, :]
        w_in_state = (xdt * jnp.exp(cum_last - cum)).astype(BF16)
        y_off = []
        st_new = []
        for gi in range(SSM_GROUPS):
            ncols = slice(gi * SSM_STATE, (gi + 1) * SSM_STATE)
            hcols = slice(gi * GROUP_W, (gi + 1) * GROUP_W)
            y_off.append(jnp.dot(cmat[:, ncols], st_b[:, hcols], preferred_element_type=F32))
            st_new.append(lax.dot_general(bmat[:, ncols], w_in_state[:, hcols], (((0,), (0,)), ((), ())),
                                          preferred_element_type=F32))
        y = y + jnp.concatenate(y_off, axis=1) * jnp.exp(cum) + xs * d_b
        st_next = st * jnp.exp(cum_last) + jnp.concatenate(st_new, axis=1)
        st_ref[...] = st_next

        yz = y * zs_ref[rows, :]
        yn = yz * lax.rsqrt(jnp.mean(yz * yz, axis=-1, keepdims=True) + EPS) * snw
        mixb_ref[rows, :] = yn.astype(BF16)

        if not streaming:
            ssm_out_ref[c] = st_next.T
            conv_out_ref[c] = xbc_ref[c, CHUNK + HIST - (CONV_W - 1):CHUNK + HIST, :]

    if streaming:
        @pl.when(tile_idx == n_tiles - 1)
        def _():
            ssm_out_ref[0] = st_ref[...].T
            conv_out_ref[0] = xbc_ref[N_CHUNKS - 1, CHUNK + HIST - (CONV_W - 1):CHUNK + HIST, :]

    o = x + oa_ref[...] + jnp.dot(mixb_ref[...], wout_ref[D_A:, :], preferred_element_type=F32)
    y_ref[...] = o * lax.rsqrt(jnp.mean(o * o, axis=-1, keepdims=True) + EPS) * nf_ref[...]


def _run_path(x, ssm_in, conv_in, weights, *, streaming):
    n_seq, seq_len, _ = x.shape
    n_rows = n_seq * seq_len
    x2 = x.reshape(n_rows, D_MODEL)
    if streaming:
        assert seq_len % TILE == 0
        tiles_per_seq = seq_len // TILE
        grid = (n_seq, tiles_per_seq)
        row_map = lambda b, i: (b * tiles_per_seq + i, 0)
        seq_map = lambda b, i: (b, 0, 0)
        const2 = lambda b, i: (0, 0)
        const3 = lambda b, i: (0, 0, 0)
        seq_blk = 1
    else:
        assert seq_len == CHUNK and n_rows % TILE == 0
        grid = (n_rows // TILE,)
        row_map = lambda i: (i, 0)
        seq_map = lambda i: (i, 0, 0)
        const2 = lambda i: (0, 0)
        const3 = lambda i: (0, 0, 0)
        seq_blk = N_CHUNKS

    def const_spec(arr):
        if arr.ndim == 2:
            return pl.BlockSpec(arr.shape, const2)
        return pl.BlockSpec(arr.shape, const3)

    in_specs = [
        pl.BlockSpec((TILE, D_MODEL), row_map),
        pl.BlockSpec((seq_blk, D_B, SSM_STATE), seq_map),
        pl.BlockSpec((seq_blk, CONV_W - 1, CONV_DIM), seq_map),
    ] + [const_spec(w) for w in weights]

    out_shape = [
        jax.ShapeDtypeStruct((n_rows, D_MODEL), F32),
        jax.ShapeDtypeStruct((n_seq, D_B, SSM_STATE), F32),
        jax.ShapeDtypeStruct((n_seq, CONV_W - 1, CONV_DIM), F32),
    ]
    out_specs = [
        pl.BlockSpec((TILE, D_MODEL), row_map),
        pl.BlockSpec((seq_blk, D_B, SSM_STATE), seq_map),
        pl.BlockSpec((seq_blk, CONV_W - 1, CONV_DIM), seq_map),
    ]
    if not streaming:
        out_shape.append(jax.ShapeDtypeStruct((n_rows, D_A), F32))
        out_specs.append(pl.BlockSpec((TILE, D_A), row_map))

    scratch_shapes = [
        pltpu.VMEM((SSM_STATE, D_B), F32),
        pltpu.VMEM((N_CHUNKS, HIST + CHUNK, CONV_DIM), F32),
        pltpu.VMEM((TILE, D_MODEL), F32),
        pltpu.VMEM((TILE, D_A), BF16),
        pltpu.VMEM((TILE, D_B), F32),
        pltpu.VMEM((TILE, D_B), F32),
        pltpu.VMEM((TILE, D_A), BF16),
        pltpu.VMEM((TILE, D_B), BF16),
    ]

    return pl.pallas_call(
        functools.partial(_mixer_kernel, streaming=streaming),
        grid=grid,
        in_specs=in_specs,
        out_specs=out_specs,
        out_shape=out_shape,
        scratch_shapes=scratch_shapes,
        compiler_params=pltpu.CompilerParams(
            dimension_semantics=("arbitrary",) * len(grid),
            vmem_limit_bytes=V7X_VMEM_LIMIT_BYTES,
        ),
        name="mixer_prompt" if streaming else "mixer_sample",
    )(x2, ssm_in, conv_in, *weights)


def kernel(x_prompt, x_sample, state_ssm, state_conv, norm_in_w, w_in, gmlp_ln_w, gmlp_ln_b, gmlp_ws, gmlp_bs,
           conv_w, conv_b, dt_bias, a_log, d_skip, ssm_norm_w, w_out, norm_f_w):
    depth = w_in.shape[0]
    assert depth == 1
    bp, seq, _ = x_prompt.shape
    bs, dec_len, _ = x_sample.shape

    def head_bcast(p):
        return jnp.repeat(p, SSM_HEAD_DIM)[None, :]

    i = 0
    w = w_in[i]
    w_dt = jnp.repeat(w[:, OFF_DT:], SSM_HEAD_DIM, axis=1)
    w_all = jnp.concatenate([w[:, :OFF_DT], w_dt], axis=1).astype(BF16)
    weights = (
        norm_in_w[i][None, :], w_all, gmlp_ln_w[i][None, :], gmlp_ln_b[i][None, :], gmlp_ws[i],
        jnp.repeat(gmlp_bs[i].T, GMLP_HEAD_DIM, axis=1),
        conv_w[i], conv_b[i][None, :], head_bcast(dt_bias[i]), head_bcast(a_log[i]), head_bcast(d_skip[i]),
        ssm_norm_w[i][None, :], w_out[i].astype(BF16), norm_f_w[None, :],
    )

    zero_ssm = jnp.zeros((bp, D_B, SSM_STATE), F32)
    zero_conv = jnp.zeros((bp, CONV_W - 1, CONV_DIM), F32)
    yp, ssm_p, conv_p = _run_path(x_prompt, zero_ssm, zero_conv, weights, streaming=True)
    ys, ssm_s, conv_s, vn_s = _run_path(
        x_sample, state_ssm[i].reshape(bs, D_B, SSM_STATE), state_conv[i], weights, streaming=False)

    return (
        yp.reshape(bp, seq, D_MODEL),
        ys.reshape(bs, dec_len, D_MODEL),
        ssm_p.reshape(1, bp, SSM_HEADS, SSM_HEAD_DIM, SSM_STATE),
        conv_p[None],
        ssm_s.reshape(1, bs, SSM_HEADS, SSM_HEAD_DIM, SSM_STATE),
        conv_s[None],
        vn_s.reshape(1, bs, dec_len, D_A),
    )
```

```python
import functools

import jax
import jax.numpy as jnp
from jax import lax
from jax.experimental import pallas as pl
from jax.experimental.pallas import tpu as pltpu

D_MODEL = 1024
CHUNK = 64
D_A = 1024
GMLP_HEADS = 8
GMLP_HEAD_DIM = 128
GMLP_CHUNK = 128
D_B = 1024
SSM_HEADS = 16
SSM_HEAD_DIM = 64
SSM_GROUPS = 2
SSM_STATE = 128
GN = SSM_GROUPS * SSM_STATE
CONV_W = 4
CONV_DIM = D_B + 2 * GN
EPS = 1e-5

V7X_LANES = 128
V7X_MXU_COLS = 256
TILE = 256
N_CHUNKS = TILE // CHUNK
V7X_SUBLANES = 8
HIST = V7X_SUBLANES
TAP0 = HIST - (CONV_W - 1)
GROUP_W = D_B // SSM_GROUPS
BLOCK_HEADS = 4
BLOCK_W = BLOCK_HEADS * SSM_HEAD_DIM
DT_PARTS = 3
CUM_ROWS = 2 * CHUNK + 2 * V7X_SUBLANES

OFF_U, OFF_V, OFF_G, OFF_Z, OFF_XBC, OFF_DT = 0, D_A, 2 * D_A, 3 * D_A, 3 * D_A + D_B, 3 * D_A + D_B + CONV_DIM

V7X_VMEM_LIMIT_BYTES = 56 * 1024 * 1024
N_WEIGHTS = 16

F32 = jnp.float32
BF16 = jnp.bfloat16


def _silu(x):
    return x * jax.nn.sigmoid(x)


def _softplus(x):
    return jnp.maximum(x, 0.0) + jnp.log1p(jnp.exp(-jnp.abs(x)))


def _rms(x):
    return x * lax.rsqrt(jnp.mean(x * x, axis=-1, keepdims=True) + EPS)


def _col_tiles(width):
    return [slice(j * V7X_MXU_COLS, (j + 1) * V7X_MXU_COLS) for j in range(width // V7X_MXU_COLS)]


ROW_GROUPS = [slice(r * CHUNK, (r + 1) * CHUNK) for r in range(N_CHUNKS)]

SSD_INPUT_TAGS = ("norm", "dt-narrow", "dt", "z", "xbc")
def _norm_pieces(x_ref, xnb_ref, nin_ref):
    for rows in ROW_GROUPS:
        yield "norm"
        xnb_ref[rows, :] = (_rms(x_ref[rows, :]) * nin_ref[...]).astype(BF16)


def _p_pieces(x_ref, xnb_ref, weights, dst, vn_out_ref, pitch, gl, norm_here, ssd_inputs_first,
              x_next_ref=None, xnb_next_ref=None):
    (nin_ref, win_ref, lnw_ref, lnb_ref, ws_ref, bsb_ref, _cw, _cb, dtb_ref, e3_ref, _alog, _dsk, _snw,
     wout_ref, _nf, wdt_ref) = weights
    dt_ref, zs_ref, xbc_ref, xo_ref, vraw_ref, vnb_ref, mixa_ref = dst

    if norm_here:
        yield from _norm_pieces(x_ref, xnb_ref, nin_ref)

    def proj(lo, width=V7X_MXU_COLS):
        return jnp.dot(xnb_ref[...], win_ref[:, lo:lo + width], preferred_element_type=F32)

    dt_parts = []

    def dt_narrow():
        dt_raw = jnp.dot(xnb_ref[...], wdt_ref[...], preferred_element_type=F32)
        dtn = _softplus(dt_raw + dtb_ref[...])
        hi = dtn.astype(BF16).astype(F32)
        r1 = dtn - hi
        mid = r1.astype(BF16).astype(F32)
        lane = lax.broadcasted_iota(jnp.int32, dtn.shape, 1)
        dt_parts.append(
            jnp.where(lane < SSM_HEADS, hi, jnp.where(lane < 2 * SSM_HEADS, mid, r1 - mid)).astype(BF16))

    def dt_bcast():
        dt_ref[...] = jnp.dot(dt_parts.pop(), e3_ref[...], preferred_element_type=F32)

    def z_proj(cols):
        zs_ref[:, cols] = _silu(proj(OFF_Z + cols.start))

    def xbc_proj(cols):
        xbc = proj(OFF_XBC + cols.start)
        for c in range(N_CHUNKS):
            xbc_ref[c * pitch + HIST:c * pitch + HIST + CHUNK, cols] = xbc[c * CHUNK:(c + 1) * CHUNK]

    def v_proj(cols):
        vraw_ref[:, cols] = proj(OFF_V + cols.start)

    def v_norm(r):
        rows = ROW_GROUPS[r]
        v = vraw_ref[rows, :]
        vc = v - jnp.mean(v, axis=-1, keepdims=True)
        var = jnp.mean(vc * vc, axis=-1, keepdims=True)
        vn = vc * lax.rsqrt(var + EPS) * lnw_ref[...] + lnb_ref[...]
        if vn_out_ref is not None:
            vn_out_ref[rows, :] = vn
        vnb_ref[rows, :] = vn.astype(BF16)

    blk_t = lax.broadcasted_iota(jnp.int32, (gl, gl), 0) // CHUNK
    blk_s = lax.broadcasted_iota(jnp.int32, (gl, gl), 1) // CHUNK
    w_keep = blk_s <= blk_t
    pair_cols = _col_tiles(D_A)
    u, ug = {}, {}

    def u_proj(j):
        u[j] = proj(OFF_U + pair_cols[j].start)

    def g_proj(j):
        ug[j] = u.pop(j) * _silu(proj(OFF_G + pair_cols[j].start))

    def gmlp(j):
        ug_j = ug.pop(j)
        for hh in range(V7X_MXU_COLS // GMLP_HEAD_DIM):
            h = j * (V7X_MXU_COLS // GMLP_HEAD_DIM) + hh
            cols = slice(h * GMLP_HEAD_DIM, (h + 1) * GMLP_HEAD_DIM)
            wm = jnp.where(w_keep, ws_ref[h, 0:gl, 0:gl], 0.0).astype(BF16)
            bias = bsb_ref[0:gl, cols]
            for k in range(TILE // gl):
                rows = slice(k * gl, (k + 1) * gl)
                mixed = jnp.dot(wm, vnb_ref[rows, cols], preferred_element_type=F32) + bias
                ug_h = ug_j[rows, hh * GMLP_HEAD_DIM:(hh + 1) * GMLP_HEAD_DIM]
                mixa_ref[rows, cols] = (ug_h * mixed).astype(BF16)

    head = [("dt-narrow", dt_narrow)]
    z = [("z", functools.partial(z_proj, cols)) for cols in _col_tiles(D_B)]
    xbc = [("xbc", functools.partial(xbc_proj, cols)) for cols in _col_tiles(CONV_DIM)]
    v = [("v", functools.partial(v_proj, cols)) for cols in _col_tiles(D_A)]
    vn = [("v-norm", functools.partial(v_norm, r)) for r in range(len(ROW_GROUPS))]
    if ssd_inputs_first:
        order = head + z + [("dt", dt_bcast)] + xbc + v + vn
    else:
        order = head + v + z + [("dt", dt_bcast)] + xbc[:2] + vn + xbc[2:]
    for tag, emit in order:
        yield tag
        emit()
    for j in range(len(pair_cols)):
        yield "u"
        u_proj(j)
        yield "g"
        g_proj(j)
        yield "gmlp"
        gmlp(j)

    if x_next_ref is not None:
        yield from _norm_pieces(x_next_ref, xnb_next_ref, nin_ref)

    for cols in _col_tiles(D_MODEL):
        yield "out-a"
        xo_ref[:, cols] = x_ref[:, cols] + jnp.dot(mixa_ref[...], wout_ref[0:D_A, cols],
                                                   preferred_element_type=F32)


def _ssd_consts(weights):
    cw_ref, cb_ref = weights[6], weights[7]
    alog_ref, dsk_ref, snw_ref = weights[10], weights[11], weights[12]
    row = lax.broadcasted_iota(jnp.int32, (CHUNK, D_B), 0)
    pos = lax.broadcasted_iota(jnp.int32, (CHUNK, D_B), 1) % SSM_HEAD_DIM
    li = lax.broadcasted_iota(jnp.int32, (CUM_ROWS, 4 * CHUNK), 0)
    ji = lax.broadcasted_iota(jnp.int32, (CUM_ROWS, 4 * CHUNK), 1)
    tri = (ji % CHUNK) <= (li % CHUNK)
    plain_blk = (ji // CHUNK) % 2 == 0
    cum_lhs = jnp.where(
        li < CHUNK, jnp.where(plain_blk, jnp.where(tri, 1.0, 0.0), -1.0),
        jnp.where(plain_blk, jnp.where(jnp.logical_or(tri, li >= 2 * CHUNK), 1.0, 0.0), 0.0))
    blk_r = lax.broadcasted_iota(jnp.int32, (BLOCK_W, BLOCK_W), 0) // SSM_HEAD_DIM
    blk_c = lax.broadcasted_iota(jnp.int32, (BLOCK_W, BLOCK_W), 1) // SSM_HEAD_DIM
    return dict(
        a_b=-jnp.exp(alog_ref[...]),
        d_b=dsk_ref[...], cw=cw_ref[...], cb=cb_ref[...], snw=snw_ref[...],
        causal=row >= pos,
        upto=jnp.where(row <= pos, 1.0, 0.0).astype(BF16),
        cum_lhs=cum_lhs.astype(BF16),
        same_head=blk_r == blk_c,
    )


def _s_pieces(k, weights, src, st_ref, mixb_ref, y_ref, out_rows, pitch, before_state=None, after_state=None):
    dt_ref, zs_ref, xbc_ref, xo_ref = src
    wout_ref, nf_ref = weights[13], weights[14]
    cw, cb = k["cw"], k["cb"]
    def conv(base, cols):
        acc = xbc_ref[base + CONV_W - 1:base + CONV_W - 1 + CHUNK, cols] * cw[CONV_W - 1:CONV_W, cols]
        for j in range(CONV_W - 2, -1, -1):
            acc = acc + xbc_ref[base + j:base + j + CHUNK, cols] * cw[j:j + 1, cols]
        return _silu(acc + cb[:, cols])

    def chunk_stages(c):
        rows = slice(c * CHUNK, (c + 1) * CHUNK)
        base = c * pitch + TAP0
        xc_cols = []
        for cols in (slice(0, D_B // 2), slice(D_B // 2, D_B), slice(D_B, CONV_DIM)):
            yield "conv"
            xc_cols.append(conv(base, cols))
        xs = jnp.concatenate(xc_cols[:2], axis=1)
        bmat = xc_cols[2][:, :GN].astype(BF16)
        cmat = xc_cols[2][:, GN:].astype(BF16)

        yield "cumsum"
        dt = dt_ref[rows, :]
        da = dt * k["a_b"]
        da_hi = da.astype(BF16)
        da_lo = (da - da_hi.astype(F32)).astype(BF16)
        cum_rhs = jnp.concatenate([da_hi, da_hi * k["upto"], da_lo, da_lo * k["upto"]], axis=0)
        seg_cum = jnp.dot(k["cum_lhs"], cum_rhs, preferred_element_type=F32)
        seg = seg_cum[:CHUNK]
        cum = seg_cum[CHUNK:2 * CHUNK]
        cum_last8 = seg_cum[2 * CHUNK:2 * CHUNK + V7X_SUBLANES]

        yield "decay"
        decay = jnp.where(k["causal"], jnp.exp(seg), 0.0)
        cb_tiled = []
        for gi in range(SSM_GROUPS):
            ncols = slice(gi * SSM_STATE, (gi + 1) * SSM_STATE)
            b_rep = jnp.concatenate([bmat[:, ncols]] * (GROUP_W // CHUNK), axis=0)
            cb_tiled.append(lax.dot_general(cmat[:, ncols], b_rep, (((1,), (1,)), ((), ())),
                                            preferred_element_type=F32))
        m = (jnp.concatenate(cb_tiled, axis=1) * decay).astype(BF16)

        yield "intra"
        xdt = xs * dt
        xdt_b = xdt.astype(BF16)
        y_parts = []
        for j in range(D_B // BLOCK_W):
            cols = slice(j * BLOCK_W, (j + 1) * BLOCK_W)
            stacked = jnp.concatenate([xdt_b[:, cols]] * BLOCK_HEADS, axis=0)
            rhs = jnp.where(k["same_head"], stacked, jnp.zeros_like(stacked))
            y_parts.append(jnp.dot(m[:, cols], rhs, preferred_element_type=F32))
        y = jnp.concatenate(y_parts, axis=1)
        cum_last = jnp.concatenate([cum_last8] * (CHUNK // V7X_SUBLANES), axis=0)
        w_in_state = (xdt * jnp.exp(cum_last - cum)).astype(BF16)

        yield "state"
        if before_state is not None:
            before_state(c)
        st = st_ref[...]
        st_b = st.astype(BF16)
        y_off = []
        st_new = []
        for gi in range(SSM_GROUPS):
            ncols = slice(gi * SSM_STATE, (gi + 1) * SSM_STATE)
            hcols = slice(gi * GROUP_W, (gi + 1) * GROUP_W)
            y_off.append(jnp.dot(cmat[:, ncols], st_b[:, hcols], preferred_element_type=F32))
            st_new.append(lax.dot_general(bmat[:, ncols], w_in_state[:, hcols], (((0,), (0,)), ((), ())),
                                          preferred_element_type=F32))
        y = y + jnp.concatenate(y_off, axis=1) * jnp.exp(cum) + xs * k["d_b"]
        chunk_decay = jnp.concatenate([jnp.exp(cum_last8)] * (SSM_STATE // V7X_SUBLANES), axis=0)
        st_next = st * chunk_decay + jnp.concatenate(st_new, axis=1)
        st_ref[...] = st_next
        if after_state is not None:
            after_state(c, st_next)

        yield "gate-norm"
        mixb_ref[rows, :] = (_rms(y * zs_ref[rows, :]) * k["snw"]).astype(BF16)

    for c in range(N_CHUNKS):
        yield from chunk_stages(c)

    for cols in _col_tiles(D_MODEL):
        yield "out-b"
        y_ref[out_rows, cols] = xo_ref[:, cols] + jnp.dot(mixb_ref[...], wout_ref[D_A:, cols],
                                                          preferred_element_type=F32)
    for rows in ROW_GROUPS:
        yield "final-norm"
        o_rows = slice(out_rows.start + rows.start, out_rows.start + rows.stop)
        y_ref[o_rows, :] = _rms(y_ref[o_rows, :]) * nf_ref[...]


def _drain(gen, head, stop_tags=()):
    while head is not None and head not in stop_tags:
        head = next(gen, None)
    return head


def _merge(p, p_head, s, s_head, s_waits_at=None):
    while p_head is not None or s_head is not None:
        if p_head is not None:
            p_head = next(p, None)
        if s_head is not None and not (s_head == s_waits_at and p_head is not None):
            s_head = next(s, None)


def _conv_tail(xbc_ref, pitch, c):
    return xbc_ref[c * pitch + HIST + CHUNK - (CONV_W - 1):c * pitch + HIST + CHUNK, :]


def _const_spec(arr):
    zeros = (0,) * arr.ndim
    return pl.BlockSpec(arr.shape, lambda s: zeros, pipeline_mode=pl.Buffered(1))


def _compiler_params():
    return pltpu.CompilerParams(
        dimension_semantics=("arbitrary",),
        vmem_limit_bytes=V7X_VMEM_LIMIT_BYTES,
    )


PROMPT_PITCH = CHUNK


def _prompt_kernel(x0_ref, xa_ref, xb_ref, xc_ref, ssm_in_ref, conv_in_ref, *rest, steps_per_seq):
    weights = rest[:N_WEIGHTS]
    y_ref, ssm_out_ref, conv_out_ref = rest[N_WEIGHTS:N_WEIGHTS + 3]
    (st_ref, dt0, zs0, xbc0, xo0, xnb0, dt1, zs1, xbc1, xo1, xnb1,
     vraw_ref, vnb_ref, mixa_ref, mixb_ref) = rest[N_WEIGHTS + 3:]
    slots = ((dt0, zs0, xbc0, xo0), (dt1, zs1, xbc1, xo1))
    xnb = (xnb0, xnb1)
    step = pl.program_id(0)
    seq_step = step % steps_per_seq
    consts = _ssd_consts(weights)

    def p_pieces(x_ref, slot, norm_here, x_next_ref):
        return _p_pieces(x_ref, xnb[slot], weights, slots[slot] + (vraw_ref, vnb_ref, mixa_ref), None,
                         PROMPT_PITCH, GMLP_CHUNK, norm_here, True, x_next_ref, xnb[1 - slot])

    @pl.when(step == 0)
    def _():
        p0 = p_pieces(x0_ref, 0, True, xa_ref)
        _drain(p0, next(p0))

    @pl.when(seq_step == 0)
    def _():
        st_ref[...] = ssm_in_ref[0].T
        xbc0[TAP0:HIST, :] = conv_in_ref[0]

    @pl.when(seq_step != 0)
    def _():
        xbc0[0:HIST, :] = xbc1[TILE:TILE + HIST, :]

    def half(x_p_ref, x_norm_ref, cur, out_rows):
        p = p_pieces(x_p_ref, 1 - cur, False, x_norm_ref)
        s = _s_pieces(consts, weights, slots[cur], st_ref, mixb_ref, y_ref, out_rows, PROMPT_PITCH)
        _merge(p, next(p), s, next(s))

    half(xa_ref, xb_ref, 0, slice(0, TILE))

    @pl.when(step >= 0)
    def _():
        xbc1[0:HIST, :] = xbc0[TILE:TILE + HIST, :]

    half(xb_ref, xc_ref, 1, slice(TILE, 2 * TILE))

    @pl.when(seq_step == steps_per_seq - 1)
    def _():
        ssm_out_ref[0] = st_ref[...].T
        conv_out_ref[0] = _conv_tail(xbc1, PROMPT_PITCH, N_CHUNKS - 1)


def _run_prompt(x, ssm_in, conv_in, weights):
    n_seq, seq_len, _ = x.shape
    assert seq_len % (2 * TILE) == 0
    n_rows = n_seq * seq_len
    n_tiles = n_rows // TILE
    steps_per_seq = seq_len // (2 * TILE)
    x2 = x.reshape(n_rows, D_MODEL)

    seq_map = lambda s: (s // steps_per_seq, 0, 0)
    in_specs = [
        pl.BlockSpec((TILE, D_MODEL), lambda s: (0, 0), pipeline_mode=pl.Buffered(1)),
        pl.BlockSpec((TILE, D_MODEL), lambda s: (2 * s + 1, 0)),
        pl.BlockSpec((TILE, D_MODEL), lambda s: (jnp.minimum(2 * s + 2, n_tiles - 1), 0)),
        pl.BlockSpec((TILE, D_MODEL), lambda s: (jnp.minimum(2 * s + 3, n_tiles - 1), 0)),
        pl.BlockSpec((1, D_B, SSM_STATE), seq_map),
        pl.BlockSpec((1, CONV_W - 1, CONV_DIM), seq_map),
    ] + [_const_spec(w) for w in weights]
    out_shape = [
        jax.ShapeDtypeStruct((n_rows, D_MODEL), F32),
        jax.ShapeDtypeStruct((n_seq, D_B, SSM_STATE), F32),
        jax.ShapeDtypeStruct((n_seq, CONV_W - 1, CONV_DIM), F32),
    ]
    out_specs = [
        pl.BlockSpec((2 * TILE, D_MODEL), lambda s: (s, 0)),
        pl.BlockSpec((1, D_B, SSM_STATE), seq_map),
        pl.BlockSpec((1, CONV_W - 1, CONV_DIM), seq_map),
    ]
    slot = [
        pltpu.VMEM((TILE, D_B), F32),
        pltpu.VMEM((TILE, D_B), F32),
        pltpu.VMEM((HIST + TILE, CONV_DIM), F32),
        pltpu.VMEM((TILE, D_MODEL), F32),
        pltpu.VMEM((TILE, D_MODEL), BF16),
    ]
    scratch_shapes = [pltpu.VMEM((SSM_STATE, D_B), F32)] + slot + slot + [
        pltpu.VMEM((TILE, D_A), F32),
        pltpu.VMEM((TILE, D_A), BF16),
        pltpu.VMEM((TILE, D_A), BF16),
        pltpu.VMEM((TILE, D_B), BF16),
    ]
    return pl.pallas_call(
        functools.partial(_prompt_kernel, steps_per_seq=steps_per_seq),
        grid=(n_tiles // 2,),
        in_specs=in_specs,
        out_specs=out_specs,
        out_shape=out_shape,
        scratch_shapes=scratch_shapes,
        compiler_params=_compiler_params(),
        name="mixer_prompt",
    )(x2, x2, x2, x2, ssm_in, conv_in, *weights)


SAMPLE_PITCH = HIST + CHUNK


def _sample_kernel(x_ref, ssm_in_ref, conv_in_ref, *rest):
    weights = rest[:N_WEIGHTS]
    y_ref, ssm_out_ref, conv_out_ref, vn_out_ref = rest[N_WEIGHTS:N_WEIGHTS + 4]
    st_ref, dt_ref, zs_ref, xbc_ref, xo_ref, xnb_ref, vraw_ref, vnb_ref, mixa_ref, mixb_ref = rest[N_WEIGHTS + 4:]
    consts = _ssd_consts(weights)

    for c in range(N_CHUNKS):
        xbc_ref[c * SAMPLE_PITCH + TAP0:c * SAMPLE_PITCH + HIST, :] = conv_in_ref[c]

    def load_state(c):
        st_ref[...] = ssm_in_ref[c].T

    def store_state(c, st_next):
        ssm_out_ref[c] = st_next.T
        conv_out_ref[c] = _conv_tail(xbc_ref, SAMPLE_PITCH, c)

    p = _p_pieces(x_ref, xnb_ref, weights, (dt_ref, zs_ref, xbc_ref, xo_ref, vraw_ref, vnb_ref, mixa_ref),
                  vn_out_ref, SAMPLE_PITCH, CHUNK, True, True)
    s = _s_pieces(consts, weights, (dt_ref, zs_ref, xbc_ref, xo_ref), st_ref, mixb_ref, y_ref,
                  slice(0, TILE), SAMPLE_PITCH, before_state=load_state, after_state=store_state)
    p_head = next(p)
    while p_head in SSD_INPUT_TAGS:
        p_head = next(p)
    _merge(p, p_head, s, next(s), s_waits_at="out-b")


def _run_sample(x, ssm_in, conv_in, weights):
    n_seq, seq_len, _ = x.shape
    assert seq_len == CHUNK and n_seq % N_CHUNKS == 0
    n_rows = n_seq * seq_len
    x2 = x.reshape(n_rows, D_MODEL)

    row_map = lambda i: (i, 0)
    seq_map = lambda i: (i, 0, 0)
    in_specs = [
        pl.BlockSpec((TILE, D_MODEL), row_map),
        pl.BlockSpec((N_CHUNKS, D_B, SSM_STATE), seq_map),
        pl.BlockSpec((N_CHUNKS, CONV_W - 1, CONV_DIM), seq_map),
    ] + [_const_spec(w) for w in weights]
    out_shape = [
        jax.ShapeDtypeStruct((n_rows, D_MODEL), F32),
        jax.ShapeDtypeStruct((n_seq, D_B, SSM_STATE), F32),
        jax.ShapeDtypeStruct((n_seq, CONV_W - 1, CONV_DIM), F32),
        jax.ShapeDtypeStruct((n_rows, D_A), F32),
    ]
    out_specs = [
        pl.BlockSpec((TILE, D_MODEL), row_map),
        pl.BlockSpec((N_CHUNKS, D_B, SSM_STATE), seq_map),
        pl.BlockSpec((N_CHUNKS, CONV_W - 1, CONV_DIM), seq_map),
        pl.BlockSpec((TILE, D_A), row_map),
    ]
    scratch_shapes = [
        pltpu.VMEM((SSM_STATE, D_B), F32),
        pltpu.VMEM((TILE, D_B), F32),
        pltpu.VMEM((TILE, D_B), F32),
        pltpu.VMEM((N_CHUNKS * SAMPLE_PITCH, CONV_DIM), F32),
        pltpu.VMEM((TILE, D_MODEL), F32),
        pltpu.VMEM((TILE, D_MODEL), BF16),
        pltpu.VMEM((TILE, D_A), F32),
        pltpu.VMEM((TILE, D_A), BF16),
        pltpu.VMEM((TILE, D_A), BF16),
        pltpu.VMEM((TILE, D_B), BF16),
    ]
    return pl.pallas_call(
        _sample_kernel,
        grid=(n_rows // TILE,),
        in_specs=in_specs,
        out_specs=out_specs,
        out_shape=out_shape,
        scratch_shapes=scratch_shapes,
        compiler_params=_compiler_params(),
        name="mixer_sample",
    )(x2, ssm_in, conv_in, *weights)


def kernel(x_prompt, x_sample, state_ssm, state_conv, norm_in_w, w_in, gmlp_ln_w, gmlp_ln_b, gmlp_ws, gmlp_bs,
           conv_w, conv_b, dt_bias, a_log, d_skip, ssm_norm_w, w_out, norm_f_w):
    depth = w_in.shape[0]
    assert depth == 1
    bp, seq, _ = x_prompt.shape
    bs, dec_len, _ = x_sample.shape

    def head_bcast(p):
        return jnp.repeat(p, SSM_HEAD_DIM)[None, :]

    def dt_lanes(p):
        pad = [(0, 0)] * (p.ndim - 1) + [(0, V7X_LANES - DT_PARTS * SSM_HEADS)]
        return jnp.pad(jnp.concatenate([p] * DT_PARTS, axis=-1), pad)

    i = 0
    w = w_in[i]
    w_main = w.astype(BF16)
    w_dt = dt_lanes(w[:, OFF_DT:]).astype(BF16)
    part_lane = jnp.arange(V7X_LANES)[:, None]
    head_of_col = jnp.arange(D_B)[None, :] // SSM_HEAD_DIM
    e3 = jnp.where((part_lane < DT_PARTS * SSM_HEADS) & (part_lane % SSM_HEADS == head_of_col), 1.0, 0.0).astype(BF16)
    weights = (
        norm_in_w[i][None, :], w_main, gmlp_ln_w[i][None, :], gmlp_ln_b[i][None, :], gmlp_ws[i],
        jnp.repeat(gmlp_bs[i].T, GMLP_HEAD_DIM, axis=1),
        conv_w[i], conv_b[i][None, :], dt_lanes(dt_bias[i])[None, :], e3,
        head_bcast(a_log[i]), head_bcast(d_skip[i]),
        ssm_norm_w[i][None, :], w_out[i].astype(BF16), norm_f_w[None, :], w_dt,
    )
    assert len(weights) == N_WEIGHTS

    zero_ssm = jnp.zeros((bp, D_B, SSM_STATE), F32)
    zero_conv = jnp.zeros((bp, CONV_W - 1, CONV_DIM), F32)
    yp, ssm_p, conv_p = _run_prompt(x_prompt, zero_ssm, zero_conv, weights)
    ys, ssm_s, conv_s, vn_s = _run_sample(
        x_sample, state_ssm[i].reshape(bs, D_B, SSM_STATE), state_conv[i], weights)

    return (
        yp.reshape(bp, seq, D_MODEL),
        ys.reshape(bs, dec_len, D_MODEL),
        ssm_p.reshape(1, bp, SSM_HEADS, SSM_HEAD_DIM, SSM_STATE),
        conv_p[None],
        ssm_s.reshape(1, bs, SSM_HEADS, SSM_HEAD_DIM, SSM_STATE),
        conv_s[None],
        vn_s.reshape(1, bs, dec_len, D_A),
    )
```
